```python
import math
import jax, jax.numpy as jnp
from jax import lax
import numpy as np

D_MODEL = 1024
BATCH = 4
SEQ = 4096
DEPTH = 4

N_MIXERS = 4
GRID_W = 64
FNET_GROUPS = 4
ATTN_Q_HEADS = 8
ATTN_KV_HEADS = 4
ATTN_HEAD_DIM = D_MODEL // ATTN_Q_HEADS
ATTN_Q_BLOCK = 128
ROPE_THETA = 10000.0
S5_GROUP_CH = 16
S5_GROUPS = D_MODEL // S5_GROUP_CH
S5_STATE = 64
HG_HEAD_DIM = 128
HG_HEADS = D_MODEL // HG_HEAD_DIM
HG_CHUNK = 64
N_EXPERTS = 32
TOP_K = 4
D_EXPERT = D_MODEL
SWIGLU_LIMIT = 7.0
SWIGLU_ALPHA = 1.702
MOE_BLOCK = 128
LN_EPS = 1e-5
RMS_EPS = 1e-6
DEEPNORM_ALPHA = (2 * DEPTH) ** 0.25
DEEPNORM_BETA = (8 * DEPTH) ** -0.25

kernel_name = "hybrid_interleaved_fnet_gqa_s5_hgrn2_moe_encoder"

F32 = jnp.float32


def _layernorm(x):
    xf = x.astype(F32)
    mu = jnp.mean(xf, -1, keepdims=True)
    var = jnp.mean(jnp.square(xf - mu), -1, keepdims=True)
    return (xf - mu) * lax.rsqrt(var + LN_EPS)


def _post_norm(z, g, b):
    return (_layernorm(z) * g.astype(F32) + b.astype(F32)).astype(z.dtype)


def _rmsnorm(x, g):
    xf = x.astype(F32)
    y = xf * lax.rsqrt(jnp.mean(jnp.square(xf), -1, keepdims=True) + RMS_EPS)
    return (y * g.astype(F32)).astype(x.dtype)


def _axial_rope_tables(L):
    rows = L // GRID_W
    row = jnp.broadcast_to(jnp.arange(rows)[:, None], (rows, GRID_W)).reshape(L).astype(F32)
    col = jnp.broadcast_to(jnp.arange(GRID_W)[None, :], (rows, GRID_W)).reshape(L).astype(F32)
    axis_dim = ATTN_HEAD_DIM // 2
    inv_freq = ROPE_THETA ** (-jnp.arange(0, axis_dim, 2, dtype=F32) / axis_dim)
    ang_r = (row[:, None] * inv_freq)[:, None, :]
    ang_c = (col[:, None] * inv_freq)[:, None, :]
    return (jnp.cos(ang_r), jnp.sin(ang_r), jnp.cos(ang_c), jnp.sin(ang_c))


def _rot_half(xp, cos, sin):
    x1, x2 = jnp.split(xp, 2, axis=-1)
    return jnp.concatenate([x1 * cos - x2 * sin, x2 * cos + x1 * sin], axis=-1)


def _apply_axial_rope(x, rope):
    cr, sr, cc, sc = rope
    xf = x.astype(F32)
    half = ATTN_HEAD_DIM // 2
    out = jnp.concatenate([_rot_half(xf[..., :half], cr, sr), _rot_half(xf[..., half:], cc, sc)], axis=-1)
    return out.astype(x.dtype)


def _fnet_mixer(h, w_out, b_out):
    B, L, D = h.shape
    hg = h.astype(F32).reshape(B, L, FNET_GROUPS, D // FNET_GROUPS)
    mixed = jnp.fft.fft2(hg, axes=(1, 3), norm="ortho").real
    return mixed.reshape(B, L, D).astype(h.dtype) @ w_out + b_out


def _attention_mixer(h, w_qkv, q_gain, k_gain, w_out, rope):
    B, L, D = h.shape
    HQ, HK, DH = ATTN_Q_HEADS, ATTN_KV_HEADS, ATTN_HEAD_DIM
    GQ = HQ // HK
    qkv = h @ w_qkv
    q = qkv[..., :HQ * DH].reshape(B, L, HQ, DH)
    k = qkv[..., HQ * DH:(HQ + HK) * DH].reshape(B, L, HK, DH)
    v = qkv[..., (HQ + HK) * DH:].reshape(B, L, HK, DH)
    q = _apply_axial_rope(_rmsnorm(q, q_gain), rope)
    k = _apply_axial_rope(_rmsnorm(k, k_gain), rope)
    q = q.reshape(B, L, HK, GQ, DH).transpose(0, 2, 3, 1, 4)
    k = k.transpose(0, 2, 1, 3)
    v = v.transpose(0, 2, 1, 3)
    nblk = L // ATTN_Q_BLOCK
    qb = q.reshape(B, HK, GQ, nblk, ATTN_Q_BLOCK, DH).transpose(3, 0, 1, 2, 4, 5)
    scale = DH ** -0.5

    def block(qi):
        s = jnp.einsum('bkgqd,bksd->bkgqs', qi, k).astype(F32) * scale
        p = jax.nn.softmax(s, axis=-1).astype(v.dtype)
        return jnp.einsum('bkgqs,bksd->bkgqd', p, v)

    o = lax.map(block, qb)
    o = o.transpose(1, 0, 4, 2, 3, 5).reshape(B, L, HQ * DH)
    return o @ w_out


def _s5_combine(e1, e2):
    a1, b1 = e1
    a2, b2 = e2
    return a1 * a2, a2 * b1 + b2


def _s5_mixer(h, a_re, a_im, log_dt, b_re, b_im, c_re, c_im, d, w_glu, w_out):
    B, L, D = h.shape
    hf = h.astype(F32)
    u = hf.reshape(B, L, S5_GROUPS, S5_GROUP_CH).astype(jnp.complex64)
    y = d.astype(F32) * hf
    for direction in range(2):
        lam = lax.complex(a_re[direction].astype(F32), a_im[direction].astype(F32))
        dt = jnp.exp(log_dt[direction].astype(F32))[:, None]
        a_bar = jnp.exp(lam * dt)
        bmat = lax.complex(b_re[direction].astype(F32), b_im[direction].astype(F32))
        b_bar = ((a_bar - 1.0) / lam)[..., None] * bmat
        bu = jnp.einsum('blgc,gpc->blgp', u, b_bar)
        a_seq = jnp.broadcast_to(a_bar[None, None], (1, L, S5_GROUPS, S5_STATE))
        _, states = lax.associative_scan(_s5_combine, (a_seq, bu), axis=1, reverse=(direction == 1))
        cmat = lax.complex(c_re[direction].astype(F32), c_im[direction].astype(F32))
        y = y + jnp.einsum('blgp,gcp->blgc', states, cmat).real.reshape(B, L, D)
    y = jax.nn.gelu(y).astype(h.dtype)
    y = y * jax.nn.sigmoid(y @ w_glu)
    return y @ w_out


def _hgrn2_chunk_scan(q, k, v, log_f):
    B, H, L, dk = q.shape
    dv = v.shape[-1]
    C = HG_CHUNK
    N = L // C
    q, k, v, log_f = [t.reshape(B, H, N, C, t.shape[-1]) for t in (q, k, v, log_f)]
    b = jnp.cumsum(log_f, axis=3)
    b_last = b[:, :, :, -1:, :]
    q_dec = q * jnp.exp(b)
    att = jnp.einsum('bhncd,bhnsd->bhncs', q_dec, k * jnp.exp(-b))
    att = jnp.where(jnp.tril(jnp.ones((C, C), bool)), att, 0.0)
    o_intra = jnp.einsum('bhncs,bhnsv->bhncv', att, v)
    kv_chunk = jnp.einsum('bhncd,bhncv->bhndv', k * jnp.exp(b_last - b), v)
    decay = jnp.exp(b_last[:, :, :, 0, :])

    def step(S, inp):
        dec, kv = inp
        return dec[..., None] * S + kv, S

    S0 = jnp.zeros((B, H, dk, dv), F32)
    _, S_prev = lax.scan(step, S0, (jnp.moveaxis(decay, 2, 0), jnp.moveaxis(kv_chunk, 2, 0)))
    o_inter = jnp.einsum('bhncd,nbhdv->bhncv', q_dec, S_prev)
    return (o_intra + o_inter).reshape(B, H, L, dv)


def _hgrn2_mixer(h, w_in, lb, norm_g, w_out):
    B, L, D = h.shape
    q, f_fw, f_bw, inp, g = jnp.split(h @ w_in, 5, axis=-1)

    def heads(t):
        return t.reshape(B, L, HG_HEADS, HG_HEAD_DIM).transpose(0, 2, 1, 3).astype(F32)

    lbh = lb.astype(F32).reshape(HG_HEADS, 1, HG_HEAD_DIM)

    def gates(fraw):
        fr = heads(fraw)
        log_f = jnp.log(lbh + (1.0 - lbh) * jax.nn.sigmoid(fr))
        k = (1.0 - lbh) * jax.nn.sigmoid(-fr)
        return k, log_f

    qh = jax.nn.silu(heads(q))
    vh = heads(inp)
    k_fw, lf_fw = gates(f_fw)
    k_bw, lf_bw = gates(f_bw)
    flip = lambda t: jnp.flip(t, axis=2)
    o_fw = _hgrn2_chunk_scan(qh, k_fw, vh, lf_fw)
    o_bw = flip(_hgrn2_chunk_scan(flip(qh), flip(k_bw), flip(vh), flip(lf_bw)))
    o = (o_fw + o_bw).transpose(0, 2, 1, 3)
    o = _rmsnorm(o, norm_g.reshape(HG_HEADS, HG_HEAD_DIM)).reshape(B, L, D).astype(h.dtype)
    return (o * jax.nn.silu(g)) @ w_out


def _moe(h, w_r, b_r, w_gu, b_gu, w_down, b_down):
    B, L, D = h.shape
    N = B * L
    t = h.reshape(N, D)
    logits = (t @ w_r + b_r).astype(F32)
    top_v, top_i = lax.top_k(logits, TOP_K)
    gates = jax.nn.softmax(top_v, axis=-1)
    e_flat = top_i.reshape(-1)
    g_flat = gates.reshape(-1)
    M = N * TOP_K
    tok = jnp.arange(M, dtype=jnp.int32) // TOP_K
    order = jnp.argsort(e_flat, stable=True)
    e_s, tok_s, g_s = e_flat[order], tok[order], g_flat[order]
    gs = jnp.bincount(e_flat, length=N_EXPERTS)
    start = jnp.cumsum(gs) - gs
    ps = ((gs + MOE_BLOCK - 1) // MOE_BLOCK) * MOE_BLOCK
    pend = jnp.cumsum(ps)
    pstart = pend - ps
    dest = pstart[e_s] + jnp.arange(M, dtype=jnp.int32) - start[e_s]
    cap = M + N_EXPERTS * MOE_BLOCK
    nb = cap // MOE_BLOCK
    buf = jnp.zeros((cap, D), t.dtype).at[dest].set(t[tok_s])
    blk_e = jnp.clip(jnp.searchsorted(pend, jnp.arange(nb) * MOE_BLOCK, side='right'), 0, N_EXPERTS - 1)

    def run_block(args):
        xb, e = args
        hgu = xb @ w_gu[e] + b_gu[e]
        gate, up = jnp.split(hgu, 2, axis=-1)
        gate = jnp.minimum(gate, SWIGLU_LIMIT)
        up = jnp.clip(up, -SWIGLU_LIMIT, SWIGLU_LIMIT)
        act = (up + 1.0) * gate * jax.nn.sigmoid(SWIGLU_ALPHA * gate)
        return act @ w_down[e] + b_down[e]

    y = lax.map(run_block, (buf.reshape(nb, MOE_BLOCK, D), blk_e)).reshape(cap, D)[dest]
    out = jnp.zeros((N, D), h.dtype).at[tok_s].add(y * g_s[:, None].astype(y.dtype))
    return out.reshape(B, L, D)


def setup_inputs(seed: int = 0) -> dict:
    key = jax.random.key(seed)
    ks = list(jax.random.split(key, 48))
    D, E, F = D_MODEL, N_EXPERTS, D_EXPERT
    G, P, CH = S5_GROUPS, S5_STATE, S5_GROUP_CH

    def nrm(shape, scale):
        return jax.random.normal(ks.pop(), shape, F32) * scale

    nA, nB, nC, nD = [len(range(m, DEPTH, N_MIXERS)) for m in range(N_MIXERS)]
    w_std = D ** -0.5
    out_std = w_std * DEEPNORM_BETA
    qkv_cols = (ATTN_Q_HEADS + 2 * ATTN_KV_HEADS) * ATTN_HEAD_DIM
    n_idx = jnp.arange(P, dtype=F32)
    return {
        "x": nrm((BATCH, SEQ, D), 1.0),
        "c": nrm((BATCH, D), 1.0),
        "ada_w": nrm((DEPTH, D, 6 * D), 0.5 * w_std),
        "ada_b": nrm((DEPTH, 6 * D), 0.02),
        "post_ln_g": 1.0 + nrm((DEPTH, 2, D), 0.02),
        "post_ln_b": nrm((DEPTH, 2, D), 0.02),
        "fnet_w_out": nrm((nA, D, D), out_std),
        "fnet_b_out": nrm((nA, D), 0.02),
        "attn_w_qkv": nrm((nB, D, qkv_cols), w_std),
        "attn_q_norm": 1.0 + nrm((nB, ATTN_HEAD_DIM), 0.02),
        "attn_k_norm": 1.0 + nrm((nB, ATTN_HEAD_DIM), 0.02),
        "attn_w_out": nrm((nB, ATTN_Q_HEADS * ATTN_HEAD_DIM, D), out_std),
        "s5_a_re": -0.5 + nrm((nC, 2, G, P), 0.01),
        "s5_a_im": jnp.pi * n_idx + nrm((nC, 2, G, P), 0.01),
        "s5_log_dt": jax.random.uniform(ks.pop(), (nC, 2, G), F32, math.log(1e-3), math.log(1e-1)),
        "s5_b_re": nrm((nC, 2, G, P, CH), (2 * CH) ** -0.5),
        "s5_b_im": nrm((nC, 2, G, P, CH), (2 * CH) ** -0.5),
        "s5_c_re": nrm((nC, 2, G, CH, P), P ** -0.5),
        "s5_c_im": nrm((nC, 2, G, CH, P), P ** -0.5),
        "s5_d": nrm((nC, D), 1.0),
        "s5_w_glu": nrm((nC, D, D), w_std),
        "s5_w_out": nrm((nC, D, D), out_std),
        "hg_w_in": nrm((nD, D, 5 * D), w_std),
        "hg_lb": nrm((DEPTH, D), 0.1),
        "hg_norm": 1.0 + nrm((nD, D), 0.02),
        "hg_w_out": nrm((nD, D, D), out_std),
        "moe_w_router": nrm((DEPTH, D, E), w_std),
        "moe_b_router": nrm((DEPTH, E), 0.01),
        "moe_w_gu": nrm((DEPTH, E, D, 2 * F), w_std),
        "moe_b_gu": nrm((DEPTH, E, 2 * F), 0.02),
        "moe_w_down": nrm((DEPTH, E, F, D), F ** -0.5 * DEEPNORM_BETA),
        "moe_b_down": nrm((DEPTH, E, D), 0.02),
    }


def reference(x, c, ada_w, ada_b, post_ln_g, post_ln_b, fnet_w_out, fnet_b_out, attn_w_qkv, attn_q_norm, attn_k_norm, attn_w_out, s5_a_re, s5_a_im, s5_log_dt, s5_b_re, s5_b_im, s5_c_re, s5_c_im, s5_d, s5_w_glu, s5_w_out, hg_w_in, hg_lb, hg_norm, hg_w_out, moe_w_router, moe_b_router, moe_w_gu, moe_b_gu, moe_w_down, moe_b_down):
    dt = x.dtype
    L = x.shape[1]
    rope = _axial_rope_tables(L)
    lb_soft = jax.nn.softmax(hg_lb.astype(F32), axis=0)
    lb_all = jnp.cumsum(lb_soft, axis=0) - lb_soft[0]
    cond = jax.nn.silu(c)
    for i in range(DEPTH):
        mod = (cond @ ada_w[i] + ada_b[i])[:, None, :]
        sh1, sc1, g1, sh2, sc2, g2 = jnp.split(mod, 6, axis=-1)
        h = (_layernorm(x) * (1.0 + sc1) + sh1).astype(dt)
        m, j = i % N_MIXERS, i // N_MIXERS
        if m == 0:
            y = _fnet_mixer(h, fnet_w_out[j], fnet_b_out[j])
        elif m == 1:
            y = _attention_mixer(h, attn_w_qkv[j], attn_q_norm[j], attn_k_norm[j], attn_w_out[j], rope)
        elif m == 2:
            y = _s5_mixer(h, s5_a_re[j], s5_a_im[j], s5_log_dt[j], s5_b_re[j], s5_b_im[j], s5_c_re[j], s5_c_im[j], s5_d[j], s5_w_glu[j], s5_w_out[j])
        else:
            y = _hgrn2_mixer(h, hg_w_in[j], lb_all[i], hg_norm[j], hg_w_out[j])
        x = _post_norm(DEEPNORM_ALPHA * x + g1 * y, post_ln_g[i, 0], post_ln_b[i, 0])
        h = (_layernorm(x) * (1.0 + sc2) + sh2).astype(dt)
        y = _moe(h, moe_w_router[i], moe_b_router[i], moe_w_gu[i], moe_b_gu[i], moe_w_down[i], moe_b_down[i])
        x = _post_norm(DEEPNORM_ALPHA * x + g2 * y, post_ln_g[i, 1], post_ln_b[i, 1])
    return x
```

```python
import functools
import math

import jax
import jax.numpy as jnp
from jax import lax
from jax.experimental import pallas as pl
from jax.experimental.pallas import tpu as pltpu

F32 = jnp.float32
BF16 = jnp.bfloat16
HIGHEST = lax.Precision.HIGHEST

D_MODEL = 1024
DEPTH = 4
N_MIXERS = 4
GRID_W = 64
FNET_GROUPS = 4
FNET_GROUP_CH = D_MODEL // FNET_GROUPS
ATTN_Q_HEADS = 8
ATTN_KV_HEADS = 4
ATTN_GQ = ATTN_Q_HEADS // ATTN_KV_HEADS
ATTN_HEAD_DIM = D_MODEL // ATTN_Q_HEADS
ROPE_THETA = 10000.0
S5_GROUP_CH = 16
S5_GROUPS = D_MODEL // S5_GROUP_CH
S5_STATE = 64
HG_HEAD_DIM = 128
HG_HEADS = D_MODEL // HG_HEAD_DIM
HG_CHUNK = 64
N_EXPERTS = 32
TOP_K = 4
SWIGLU_LIMIT = 7.0
SWIGLU_ALPHA = 1.702
LN_EPS = 1e-5
RMS_EPS = 1e-6
DEEPNORM_ALPHA = (2 * DEPTH) ** 0.25

LANES = 128
SUBLANES = 8
VMEM_LIMIT = 56 * 1024 * 1024

ROW_TILE = 256
MOE_ROWS = 256
S5_TILE_GROUPS = 8
S5_TIME_CHUNK = 256
ROUTE_LANES = 512


def _params(sem=None):
    return pltpu.CompilerParams(dimension_semantics=sem, vmem_limit_bytes=VMEM_LIMIT)


def _bdot(a, b):
    return jnp.dot(a.astype(BF16), b.astype(BF16), preferred_element_type=F32)


def _dot_nt(a, b, **kw):
    return lax.dot_general(a, b, (((1,), (1,)), ((), ())), preferred_element_type=F32, **kw)


def _dot_tn(a, b, **kw):
    return lax.dot_general(a, b, (((0,), (0,)), ((), ())), preferred_element_type=F32, **kw)


def _layernorm(x):
    mu = jnp.mean(x, -1, keepdims=True)
    xc = x - mu
    var = jnp.mean(xc * xc, -1, keepdims=True)
    return xc * lax.rsqrt(var + LN_EPS)


def _sigmoid(x):
    return 1.0 / (1.0 + jnp.exp(-x))


def _mod_kernel(c_ref, w_ref, b_ref, o_ref):
    c = c_ref[...]
    cond = c * _sigmoid(c)
    o_ref[0] = jnp.dot(cond, w_ref[0], precision=HIGHEST, preferred_element_type=F32) + b_ref[0]


def _modulation(c, ada_w, ada_b):
    B, D = c.shape
    rows = -(-B // SUBLANES) * SUBLANES
    cp = jnp.pad(c, ((0, rows - B), (0, 0)))
    tn = 1536
    out = pl.pallas_call(
        _mod_kernel,
        grid=(DEPTH, 6 * D // tn),
        in_specs=[
            pl.BlockSpec((rows, D), lambda i, j: (0, 0)),
            pl.BlockSpec((1, D, tn), lambda i, j: (i, 0, j)),
            pl.BlockSpec((1, 1, tn), lambda i, j: (i, 0, j)),
        ],
        out_specs=pl.BlockSpec((1, rows, tn), lambda i, j: (i, 0, j)),
        out_shape=jax.ShapeDtypeStruct((DEPTH, rows, 6 * D), F32),
        compiler_params=_params(("parallel", "parallel")),
    )(cp, ada_w, ada_b.reshape(DEPTH, 1, 6 * D))
    return out[:, :B].reshape(DEPTH, B, 6, 1, D)


def _tail(y, x_ref, gate_ref, lng_ref, lnb_ref, sc_ref, sh_ref):
    x_new = _layernorm(DEEPNORM_ALPHA * x_ref[...] + gate_ref[0] * y) * lng_ref[...] + lnb_ref[...]
    h = _layernorm(x_new) * (1.0 + sc_ref[0]) + sh_ref[0]
    return x_new, h


def _router_logits(h, wrt_ref, br_ref):
    return _dot_nt(wrt_ref[...], h, precision=HIGHEST) + br_ref[...]


_TAIL_NAMES = ("x", "gate", "lng", "lnb", "sc", "sh")


def _tail_specs(tm, tiles_per_batch, D):
    row = pl.BlockSpec((tm, D), lambda i: (i, 0))
    per_batch = pl.BlockSpec((1, 1, D), lambda i: (i // tiles_per_batch, 0, 0))
    vec = pl.BlockSpec((1, D), lambda i: (0, 0))
    return [row, per_batch, vec, vec, per_batch, per_batch]


def _router_specs(D):
    return [pl.BlockSpec((N_EXPERTS, D), lambda i: (0, 0)), pl.BlockSpec((N_EXPERTS, 1), lambda i: (0, 0))]


def _mixer_out_specs(tm, D):
    return [
        pl.BlockSpec((tm, D), lambda i: (i, 0)),
        pl.BlockSpec((tm, D), lambda i: (i, 0)),
        pl.BlockSpec((N_EXPERTS, tm), lambda i: (0, i)),
    ]


def _mixer_out_shapes(N, D):
    return [
        jax.ShapeDtypeStruct((N, D), F32),
        jax.ShapeDtypeStruct((N, D), F32),
        jax.ShapeDtypeStruct((N_EXPERTS, N), F32),
    ]


def _finish_mixer(y, tail_refs, wrt_ref, br_ref, xo_ref, ho_ref, lo_ref):
    x_new, h = _tail(y, *tail_refs)
    xo_ref[...] = x_new
    ho_ref[...] = h
    lo_ref[...] = _router_logits(h, wrt_ref, br_ref)


class _Tail:
    def __init__(self, x2, gate, lng, lnb, sc, sh):
        self.args = [x2, gate, lng.reshape(1, -1), lnb.reshape(1, -1), sc, sh]


def _ln_mod_kernel(x_ref, sc_ref, sh_ref, o_ref):
    o_ref[...] = (_layernorm(x_ref[...]) * (1.0 + sc_ref[0]) + sh_ref[0]).astype(o_ref.dtype)


def _ln_mod(x2, sc, sh, L):
    N, D = x2.shape
    tm = min(ROW_TILE, L)
    tpb = L // tm
    per_batch = pl.BlockSpec((1, 1, D), lambda i: (i // tpb, 0, 0))
    return pl.pallas_call(
        _ln_mod_kernel,
        grid=(N // tm,),
        in_specs=[pl.BlockSpec((tm, D), lambda i: (i, 0)), per_batch, per_batch],
        out_specs=pl.BlockSpec((tm, D), lambda i: (i, 0)),
        out_shape=jax.ShapeDtypeStruct((N, D), BF16),
        compiler_params=_params(("parallel",)),
    )(x2, sc, sh)


def _outproj_kernel(a_ref, w_ref, b_ref, *refs):
    tail_refs, (wrt_ref, br_ref, xo_ref, ho_ref, lo_ref) = refs[:6], refs[6:]
    y = jnp.dot(a_ref[...], w_ref[...], preferred_element_type=F32) + b_ref[...]
    _finish_mixer(y, tail_refs, wrt_ref, br_ref, xo_ref, ho_ref, lo_ref)


def _outproj(a, w_out, bias, tail, wrt, br, L):
    N, D = a.shape
    tm = min(ROW_TILE, L)
    return pl.pallas_call(
        _outproj_kernel,
        grid=(N // tm,),
        in_specs=[
            pl.BlockSpec((tm, D), lambda i: (i, 0)),
            pl.BlockSpec((D, D), lambda i: (0, 0)),
            pl.BlockSpec((1, D), lambda i: (0, 0)),
        ] + _tail_specs(tm, L // tm, D) + _router_specs(D),
        out_specs=_mixer_out_specs(tm, D),
        out_shape=_mixer_out_shapes(N, D),
        compiler_params=_params(("parallel",)),
    )(a, w_out.astype(BF16), bias.reshape(1, D), *tail.args, wrt, br)


def _dft_tables(L):
    a = jnp.arange(L // GRID_W, dtype=jnp.int32)[:, None]
    b = jnp.arange(GRID_W, dtype=jnp.int32)[:, None]
    lp = jnp.arange(L, dtype=jnp.int32)[None, :]
    t1 = ((GRID_W * a * lp) % L).astype(F32) * (2.0 * math.pi / L)
    t2 = ((b * lp) % L).astype(F32) * (2.0 * math.pi / L)
    c1, s1, c2, s2 = jnp.cos(t1)[:, None], jnp.sin(t1)[:, None], jnp.cos(t2)[None], jnp.sin(t2)[None]
    scale = L ** -0.5
    cl = ((c1 * c2 - s1 * s2) * scale).reshape(L, L).astype(BF16)
    sl_neg = ((s1 * c2 + c1 * s2) * -scale).reshape(L, L).astype(BF16)
    return cl, sl_neg


def _group_dft_tables():
    n = FNET_GROUP_CH
    k = (jnp.arange(n, dtype=jnp.int32)[:, None] * jnp.arange(n, dtype=jnp.int32)[None, :]) % n
    ang = k.astype(F32) * (2.0 * math.pi / n)
    scale = n ** -0.5
    return (jnp.cos(ang) * scale).astype(BF16), (jnp.sin(ang) * scale).astype(BF16)


def _fnet_uv_kernel(h_ref, cc_ref, sc_ref, u_ref, v_ref):
    h = h_ref[...]
    u_ref[...] = jnp.dot(h, cc_ref[...], preferred_element_type=F32).astype(BF16)
    v_ref[...] = jnp.dot(h, sc_ref[...], preferred_element_type=F32).astype(BF16)


def _fnet_mix_kernel(cl_ref, sl_ref, u_ref, v_ref, w_ref, b_ref, *refs):
    tail_refs, (wrt_ref, br_ref, xo_ref, ho_ref, lo_ref) = refs[:6], refs[6:]
    mixed = (jnp.dot(cl_ref[...], u_ref[0], preferred_element_type=F32)
             + jnp.dot(sl_ref[...], v_ref[0], preferred_element_type=F32))
    y = jnp.dot(mixed.astype(BF16), w_ref[...], preferred_element_type=F32) + b_ref[...]
    _finish_mixer(y, tail_refs, wrt_ref, br_ref, xo_ref, ho_ref, lo_ref)


def _fnet_mixer(h, B, L, w_out, b_out, tail, wrt, br):
    N, D = h.shape
    tm = min(ROW_TILE, L)
    cc, sc = _group_dft_tables()
    gc = FNET_GROUP_CH
    u, v = pl.pallas_call(
        _fnet_uv_kernel,
        grid=(N // tm, FNET_GROUPS),
        in_specs=[
            pl.BlockSpec((tm, gc), lambda i, g: (i, g)),
            pl.BlockSpec((gc, gc), lambda i, g: (0, 0)),
            pl.BlockSpec((gc, gc), lambda i, g: (0, 0)),
        ],
        out_specs=[pl.BlockSpec((tm, gc), lambda i, g: (i, g))] * 2,
        out_shape=[jax.ShapeDtypeStruct((N, D), BF16)] * 2,
        compiler_params=_params(("parallel", "parallel")),
    )(h, cc, sc)
    cl, sl_neg = _dft_tables(L)
    tpb = L // tm
    resident = pl.BlockSpec((1, L, D), lambda i: (i // tpb, 0, 0), pipeline_mode=pl.Buffered(1))
    return pl.pallas_call(
        _fnet_mix_kernel,
        grid=(N // tm,),
        in_specs=[
            pl.BlockSpec((tm, L), lambda i: (i % tpb, 0)),
            pl.BlockSpec((tm, L), lambda i: (i % tpb, 0)),
            resident,
            resident,
            pl.BlockSpec((D, D), lambda i: (0, 0)),
            pl.BlockSpec((1, D), lambda i: (0, 0)),
        ] + _tail_specs(tm, tpb, D) + _router_specs(D),
        out_specs=_mixer_out_specs(tm, D),
        out_shape=_mixer_out_shapes(N, D),
        compiler_params=_params(("arbitrary",)),
    )(cl, sl_neg, u.reshape(B, L, D), v.reshape(B, L, D), w_out.astype(BF16), b_out.reshape(1, D),
      *tail.args, wrt, br)


def _rope_tables(L):
    pos = jnp.arange(L, dtype=jnp.int32)
    row = (pos // GRID_W).astype(F32)[:, None]
    col = (pos % GRID_W).astype(F32)[:, None]
    axis_dim = ATTN_HEAD_DIM // 2
    inv_freq = ROPE_THETA ** (-jnp.arange(0, axis_dim, 2, dtype=F32) / axis_dim)
    ang_r, ang_c = row * inv_freq, col * inv_freq
    cos = jnp.concatenate([jnp.cos(ang_r)] * 2 + [jnp.cos(ang_c)] * 2, axis=-1)
    sin = jnp.concatenate([-jnp.sin(ang_r), jnp.sin(ang_r), -jnp.sin(ang_c), jnp.sin(ang_c)], axis=-1)
    return cos, sin


def _qkv_kernel(h_ref, w_ref, qg_ref, kg_ref, cos_ref, sin_ref, o_ref):
    dh = ATTN_HEAD_DIM
    quarter = dh // 4
    qkv = jnp.dot(h_ref[...], w_ref[...], preferred_element_type=F32)
    cos, sin = cos_ref[...], sin_ref[...]
    lane = lax.broadcasted_iota(jnp.int32, cos.shape, 1)
    first_half = (lane % (2 * quarter)) < quarter
    n_rot = ATTN_Q_HEADS + ATTN_KV_HEADS
    for head in range(n_rot):
        xh = qkv[:, head * dh:(head + 1) * dh]
        gain = qg_ref[...] if head < ATTN_Q_HEADS else kg_ref[...]
        xn = xh * lax.rsqrt(jnp.mean(xh * xh, -1, keepdims=True) + RMS_EPS) * gain
        partner = jnp.where(first_half, pltpu.roll(xn, dh - quarter, 1), pltpu.roll(xn, quarter, 1))
        xr = xn * cos + partner * sin
        if head < ATTN_Q_HEADS:
            xr = xr * (dh ** -0.5)
        o_ref[:, head * dh:(head + 1) * dh] = xr.astype(BF16)
    o_ref[:, n_rot * dh:] = qkv[:, n_rot * dh:].astype(BF16)


def _flash_kernel(q_ref, k_ref, v_ref, o_ref):
    dh = ATTN_HEAD_DIM
    k = k_ref[0]
    v = v_ref[0]
    for g in range(ATTN_GQ):
        q = q_ref[0, :, g * dh:(g + 1) * dh]
        s = _dot_nt(q, k)
        p = jnp.exp(s - jnp.max(s, -1, keepdims=True))
        denom = jnp.sum(p, -1, keepdims=True)
        o = jnp.dot(p.astype(BF16), v, preferred_element_type=F32) / denom
        o_ref[0, :, g * dh:(g + 1) * dh] = o.astype(BF16)


def _attention_mixer(h, B, L, w_qkv, q_gain, k_gain, w_out, tail, wrt, br):
    N, D = h.shape
    dh = ATTN_HEAD_DIM
    tm = min(ROW_TILE, L)
    tpb = L // tm
    cols = w_qkv.shape[1]
    cos, sin = _rope_tables(L)
    qkv = pl.pallas_call(
        _qkv_kernel,
        grid=(N // tm,),
        in_specs=[
            pl.BlockSpec((tm, D), lambda i: (i, 0)),
            pl.BlockSpec((D, cols), lambda i: (0, 0)),
            pl.BlockSpec((1, dh), lambda i: (0, 0)),
            pl.BlockSpec((1, dh), lambda i: (0, 0)),
            pl.BlockSpec((tm, dh), lambda i: (i % tpb, 0)),
            pl.BlockSpec((tm, dh), lambda i: (i % tpb, 0)),
        ],
        out_specs=pl.BlockSpec((tm, cols), lambda i: (i, 0)),
        out_shape=jax.ShapeDtypeStruct((N, cols), BF16),
        compiler_params=_params(("parallel",)),
    )(h, w_qkv.astype(BF16), q_gain.reshape(1, dh), k_gain.reshape(1, dh), cos, sin)
    qkv3 = qkv.reshape(B, L, cols)
    tq = min(ROW_TILE, L)
    k_block0 = ATTN_Q_HEADS
    v_block0 = ATTN_Q_HEADS + ATTN_KV_HEADS
    o = pl.pallas_call(
        _flash_kernel,
        grid=(B, ATTN_KV_HEADS, L // tq),
        in_specs=[
            pl.BlockSpec((1, tq, ATTN_GQ * dh), lambda b, k, i: (b, i, k)),
            pl.BlockSpec((1, L, dh), lambda b, k, i: (b, 0, k_block0 + k)),
            pl.BlockSpec((1, L, dh), lambda b, k, i: (b, 0, v_block0 + k)),
        ],
        out_specs=pl.BlockSpec((1, tq, ATTN_GQ * dh), lambda b, k, i: (b, i, k)),
        out_shape=jax.ShapeDtypeStruct((B, L, D), BF16),
        compiler_params=_params(("parallel", "parallel", "parallel")),
    )(qkv3, qkv3, qkv3)
    return _outproj(o.reshape(N, D), w_out, jnp.zeros((D,), F32), tail, wrt, br, L)


def _s5_discretise(a_re, a_im, log_dt, b_re, b_im, c_re, c_im):
    G, P, CH = S5_GROUPS, S5_STATE, S5_GROUP_CH
    GT = S5_TILE_GROUPS
    T = G // GT
    lam = lax.complex(a_re.astype(F32), a_im.astype(F32))
    dt = jnp.exp(log_dt.astype(F32))[..., None]
    a_bar = jnp.exp(lam * dt)
    b_bar = ((a_bar - 1.0) / lam)[..., None] * lax.complex(b_re.astype(F32), b_im.astype(F32))
    eye = jnp.eye(GT, dtype=F32)

    def pack_b(part):
        p = part.reshape(2, T, GT, P, CH)
        return jnp.einsum('dtgpc,gh->dtgchp', p, eye).reshape(2, T, GT * CH, GT * P)

    def pack_c(part):
        p = part.reshape(2, T, GT, CH, P)
        return jnp.einsum('dtgcp,gh->dtgphc', p, eye).reshape(2, T, GT * P, GT * CH)

    bmat = jnp.concatenate([pack_b(jnp.real(b_bar)), pack_b(jnp.imag(b_bar))], axis=-1).astype(BF16)
    cmat = jnp.concatenate([pack_c(c_re.astype(F32)), pack_c(-c_im.astype(F32))], axis=-2).astype(BF16)
    avec = jnp.concatenate([jnp.real(a_bar).reshape(2, T, GT * P), jnp.imag(a_bar).reshape(2, T, GT * P)], -1)
    return bmat, cmat, avec


def _s5_scan_kernel(nb, hf_ref, hb_ref, bf_ref, bb_ref, cf_ref, cb_ref, a_ref, yf_ref, yb_ref, bu_ref, st_ref):
    tc = hf_ref.shape[1]
    half = a_ref.shape[-1] // 2
    chains = 2 * nb

    @pl.when(pl.program_id(1) == 0)
    def _():
        st_ref[...] = jnp.zeros_like(st_ref)

    r = lax.broadcasted_iota(jnp.int32, (tc, tc), 0)
    c = lax.broadcasted_iota(jnp.int32, (tc, tc), 1)
    flip = (r + c == tc - 1).astype(BF16)

    n_lt = bu_ref.shape[0]
    hl = n_lt // 2

    def put_chain(chain, vals):
        for j in range(n_lt):
            bu_ref[j, pl.ds(chain, tc, stride=chains), :] = vals[:, j * LANES:(j + 1) * LANES]

    def get_chain(chain):
        return jnp.concatenate([bu_ref[j, pl.ds(chain, tc, stride=chains), :] for j in range(n_lt)], axis=-1)

    for b in range(nb):
        put_chain(b, jnp.dot(hf_ref[b], bf_ref[0, 0], preferred_element_type=F32))
        h_rev = jnp.dot(flip, hb_ref[b], preferred_element_type=F32).astype(BF16)
        put_chain(nb + b, jnp.dot(h_rev, bb_ref[0, 0], preferred_element_type=F32))

    a = a_ref[0]
    ar = jnp.stack([a[:, j * LANES:(j + 1) * LANES] for j in range(hl)])
    ai = jnp.stack([a[:, half + j * LANES:half + (j + 1) * LANES] for j in range(hl)])

    def step(i, carry):
        sr, si = carry
        rows = pl.ds(pl.multiple_of(i * chains, chains), chains)
        x = bu_ref[:, rows, :]
        nr = ar * sr - ai * si + x[:hl]
        ni = ar * si + ai * sr + x[hl:]
        bu_ref[:hl, rows, :] = nr
        bu_ref[hl:, rows, :] = ni
        return nr, ni

    sr, si = lax.fori_loop(0, tc, step, (st_ref[:hl], st_ref[hl:]))
    st_ref[:hl] = sr
    st_ref[hl:] = si

    for b in range(nb):
        yf_ref[b] = jnp.dot(get_chain(b).astype(BF16), cf_ref[0, 0], preferred_element_type=F32)
        sb = jnp.dot(flip, get_chain(nb + b).astype(BF16), preferred_element_type=F32).astype(BF16)
        yb_ref[b] = jnp.dot(sb, cb_ref[0, 0], preferred_element_type=F32)


def _s5_glu_kernel(h_ref, yf_ref, yb_ref, d_ref, wg_ref, w_ref, b_ref, *refs):
    tail_refs, (wrt_ref, br_ref, xo_ref, ho_ref, lo_ref) = refs[:6], refs[6:]
    y = d_ref[...] * h_ref[...].astype(F32) + yf_ref[...] + yb_ref[...]
    y = jax.nn.gelu(y)
    y = y * _sigmoid(jnp.dot(y.astype(BF16), wg_ref[...], preferred_element_type=F32))
    out = jnp.dot(y.astype(BF16), w_ref[...], preferred_element_type=F32) + b_ref[...]
    _finish_mixer(out, tail_refs, wrt_ref, br_ref, xo_ref, ho_ref, lo_ref)


def _s5_mixer(h, B, L, a_re, a_im, log_dt, b_re, b_im, c_re, c_im, d, w_glu, w_out, tail, wrt, br):
    N, D = h.shape
    assert 2 * B == SUBLANES, "the S5 scan packs (direction, batch) onto the 8 sublanes"
    bmat, cmat, avec = _s5_discretise(a_re, a_im, log_dt, b_re, b_im, c_re, c_im)
    T = bmat.shape[1]
    ch = S5_TILE_GROUPS * S5_GROUP_CH
    ns = 2 * S5_TILE_GROUPS * S5_STATE
    a8 = jnp.repeat(avec, B, axis=0).transpose(1, 0, 2)
    tc = min(S5_TIME_CHUNK, L)
    K = L // tc
    h3 = h.reshape(B, L, D)
    yf, yb = pl.pallas_call(
        functools.partial(_s5_scan_kernel, B),
        grid=(T, K),
        in_specs=[
            pl.BlockSpec((B, tc, ch), lambda t, k: (0, k, t)),
            pl.BlockSpec((B, tc, ch), lambda t, k: (0, K - 1 - k, t)),
            pl.BlockSpec((1, 1, ch, ns), lambda t, k: (0, t, 0, 0)),
            pl.BlockSpec((1, 1, ch, ns), lambda t, k: (1, t, 0, 0)),
            pl.BlockSpec((1, 1, ns, ch), lambda t, k: (0, t, 0, 0)),
            pl.BlockSpec((1, 1, ns, ch), lambda t, k: (1, t, 0, 0)),
            pl.BlockSpec((1, 2 * B, ns), lambda t, k: (t, 0, 0)),
        ],
        out_specs=[
            pl.BlockSpec((B, tc, ch), lambda t, k: (0, k, t)),
            pl.BlockSpec((B, tc, ch), lambda t, k: (0, K - 1 - k, t)),
        ],
        out_shape=[jax.ShapeDtypeStruct((B, L, D), F32)] * 2,
        scratch_shapes=[pltpu.VMEM((ns // LANES, 2 * B * tc, LANES), F32),
                        pltpu.VMEM((ns // LANES, 2 * B, LANES), F32)],
        compiler_params=_params(("parallel", "arbitrary")),
    )(h3, h3, bmat, bmat, cmat, cmat, a8)
    tm = min(ROW_TILE, L)
    row = pl.BlockSpec((tm, D), lambda i: (i, 0))
    full = pl.BlockSpec((D, D), lambda i: (0, 0))
    vec = pl.BlockSpec((1, D), lambda i: (0, 0))
    return pl.pallas_call(
        _s5_glu_kernel,
        grid=(N // tm,),
        in_specs=[row, row, row, vec, full, full, vec] + _tail_specs(tm, L // tm, D) + _router_specs(D),
        out_specs=_mixer_out_specs(tm, D),
        out_shape=_mixer_out_shapes(N, D),
        compiler_params=_params(("parallel",)),
    )(h, yf.reshape(N, D), yb.reshape(N, D), d.reshape(1, D), w_glu.astype(BF16), w_out.astype(BF16),
      jnp.zeros((1, D), F32), *tail.args, wrt, br)


def _hg_scan_kernel(q_ref, ff_ref, fb_ref, v_ref, lb_ref, o_ref, stf_ref, stb_ref):
    C = HG_CHUNK
    L = q_ref.shape[1]
    n_chunks = L // C
    lb = lb_ref[0]
    r = lax.broadcasted_iota(jnp.int32, (C, C), 0)
    c = lax.broadcasted_iota(jnp.int32, (C, C), 1)
    lower = c <= r
    upper = c >= r
    tri_f = lower.astype(F32)
    tri_b = upper.astype(F32)
    o_ref[...] = jnp.zeros_like(o_ref)
    stf_ref[...] = jnp.zeros_like(stf_ref)
    stb_ref[...] = jnp.zeros_like(stb_ref)

    def one_direction(rows, f_ref, tri, mask, edge, st_ref):
        fr = f_ref[0, rows, :]
        qh = q_ref[0, rows, :]
        qh = qh * _sigmoid(qh)
        vh = v_ref[0, rows, :].astype(BF16)
        log_f = jnp.log(lb + (1.0 - lb) * _sigmoid(fr))
        kk = (1.0 - lb) * _sigmoid(-fr)
        bcum = jnp.dot(tri, log_f, precision=HIGHEST, preferred_element_type=F32)
        b_edge = bcum[edge:edge + 1]
        q_dec = (qh * jnp.exp(bcum)).astype(BF16)
        k_dec = (kk * jnp.exp(-bcum)).astype(BF16)
        k_end = (kk * jnp.exp(b_edge - bcum)).astype(BF16)
        att = jnp.where(mask, _dot_nt(q_dec, k_dec), 0.0).astype(BF16)
        st = st_ref[...]
        o = jnp.dot(att, vh, preferred_element_type=F32) + _dot_nt(q_dec, st.astype(BF16))
        o_ref[0, rows, :] += o
        st_ref[...] = st * jnp.exp(b_edge) + _dot_tn(vh, k_end)

    def body(n, carry):
        rows_f = pl.ds(pl.multiple_of(n * C, C), C)
        rows_b = pl.ds(pl.multiple_of((n_chunks - 1 - n) * C, C), C)
        one_direction(rows_f, ff_ref, tri_f, lower, C - 1, stf_ref)
        one_direction(rows_b, fb_ref, tri_b, upper, 0, stb_ref)
        return carry

    lax.fori_loop(0, n_chunks, body, 0)


def _hg_out_kernel(o_ref, g_ref, gain_ref, w_ref, b_ref, *refs):
    tail_refs, (wrt_ref, br_ref, xo_ref, ho_ref, lo_ref) = refs[:6], refs[6:]
    dh = HG_HEAD_DIM
    g = g_ref[...]
    sg = g * _sigmoid(g)
    parts = []
    for hd in range(HG_HEADS):
        oh = o_ref[:, hd * dh:(hd + 1) * dh]
        on = oh * lax.rsqrt(jnp.mean(oh * oh, -1, keepdims=True) + RMS_EPS) * gain_ref[:, hd * dh:(hd + 1) * dh]
        parts.append((on * sg[:, hd * dh:(hd + 1) * dh]).astype(BF16))
    a = jnp.concatenate(parts, axis=-1)
    y = jnp.dot(a, w_ref[...], preferred_element_type=F32) + b_ref[...]
    _finish_mixer(y, tail_refs, wrt_ref, br_ref, xo_ref, ho_ref, lo_ref)


def _matmul_kernel(a_ref, w_ref, o_ref):
    o_ref[...] = jnp.dot(a_ref[...], w_ref[...], preferred_element_type=F32).astype(o_ref.dtype)


def _hgrn2_mixer(h, B, L, w_in, lb, norm_g, w_out, tail, wrt, br):
    N, D = h.shape
    tm = min(ROW_TILE, L)
    cols = w_in.shape[1]
    tn = D
    proj = pl.pallas_call(
        _matmul_kernel,
        grid=(cols // tn, N // tm),
        in_specs=[pl.BlockSpec((tm, D), lambda j, i: (i, 0)), pl.BlockSpec((D, tn), lambda j, i: (0, j))],
        out_specs=pl.BlockSpec((tm, tn), lambda j, i: (i, j)),
        out_shape=jax.ShapeDtypeStruct((N, cols), F32),
        compiler_params=_params(("parallel", "parallel")),
    )(h, w_in.astype(BF16))
    proj3 = proj.reshape(B, L, cols)
    dh = HG_HEAD_DIM
    H = HG_HEADS

    def head_cols(part):
        return pl.BlockSpec((1, L, dh), lambda b, hd: (b, 0, part * H + hd))

    o = pl.pallas_call(
        _hg_scan_kernel,
        grid=(B, H),
        in_specs=[head_cols(0), head_cols(1), head_cols(2), head_cols(3),
                  pl.BlockSpec((1, 1, dh), lambda b, hd: (hd, 0, 0))],
        out_specs=pl.BlockSpec((1, L, dh), lambda b, hd: (b, 0, hd)),
        out_shape=jax.ShapeDtypeStruct((B, L, D), F32),
        scratch_shapes=[pltpu.VMEM((dh, dh), F32), pltpu.VMEM((dh, dh), F32)],
        compiler_params=_params(("parallel", "parallel")),
    )(proj3, proj3, proj3, proj3, lb.reshape(H, 1, dh))
    row = pl.BlockSpec((tm, D), lambda i: (i, 0))
    vec = pl.BlockSpec((1, D), lambda i: (0, 0))
    return pl.pallas_call(
        _hg_out_kernel,
        grid=(N // tm,),
        in_specs=[row, pl.BlockSpec((tm, D), lambda i: (i, 4)), vec, pl.BlockSpec((D, D), lambda i: (0, 0)), vec]
        + _tail_specs(tm, L // tm, D) + _router_specs(D),
        out_specs=_mixer_out_specs(tm, D),
        out_shape=_mixer_out_shapes(N, D),
        compiler_params=_params(("parallel",)),
    )(o.reshape(N, D), proj, norm_g.reshape(1, D), w_out.astype(BF16), jnp.zeros((1, D), F32),
      *tail.args, wrt, br)


def _route_kernel(n_blocks, lg_ref, ti_ref, gt_ref, dest_ref, meta_ref, rank_ref):
    E, N = lg_ref.shape
    W = min(ROUTE_LANES, N)
    steps = N // W
    eidx = lax.broadcasted_iota(jnp.int32, (E, W), 0)
    r = lax.broadcasted_iota(jnp.int32, (W, W), 0)
    c = lax.broadcasted_iota(jnp.int32, (W, W), 1)
    before = (r < c).astype(BF16)

    def count_step(j, carry):
        cols = pl.ds(pl.multiple_of(j * W, W), W)
        lg = lg_ref[:, cols]
        chosen = jnp.zeros((E, W), jnp.bool_)
        hits, vals = [], []
        for k in range(TOP_K):
            m = jnp.max(lg, axis=0, keepdims=True)
            idx = jnp.min(jnp.where(lg == m, eidx, E), axis=0, keepdims=True)
            hit = eidx == idx
            hits.append(hit)
            vals.append(m)
            ti_ref[k:k + 1, cols] = idx
            lg = jnp.where(hit, -jnp.inf, lg)
            chosen = jnp.logical_or(chosen, hit)
        ex = [jnp.exp(v - vals[0]) for v in vals]
        denom = ex[0] + ex[1] + ex[2] + ex[3]
        for k in range(TOP_K):
            gt_ref[k:k + 1, cols] = ex[k] / denom
        onehot = chosen.astype(BF16)
        rank = jnp.dot(onehot, before, preferred_element_type=F32) + carry
        for k in range(TOP_K):
            rank_ref[k:k + 1, cols] = jnp.sum(jnp.where(hits[k], rank, 0.0), axis=0, keepdims=True)
        return carry + jnp.sum(chosen.astype(F32), axis=1, keepdims=True)

    counts = lax.fori_loop(0, steps, count_step, jnp.zeros((E, 1), F32))
    padded = jnp.ceil(counts / MOE_ROWS) * MOE_ROWS
    er = lax.broadcasted_iota(jnp.int32, (E, E), 0)
    ec = lax.broadcasted_iota(jnp.int32, (E, E), 1)
    incl = (ec <= er).astype(F32)
    pend = jnp.dot(incl, jnp.broadcast_to(padded, (E, LANES)), precision=HIGHEST, preferred_element_type=F32)
    pend = pend[:, :1]
    pstart = pend - padded

    def dest_step(j, carry):
        cols = pl.ds(pl.multiple_of(j * W, W), W)
        for k in range(TOP_K):
            base = jnp.sum(jnp.where(eidx == ti_ref[k:k + 1, cols], pstart, 0.0), axis=0, keepdims=True)
            dest_ref[k:k + 1, cols] = (base + rank_ref[k:k + 1, cols]).astype(jnp.int32)
        return carry

    lax.fori_loop(0, steps, dest_step, 0)
    nbp = meta_ref.shape[1]
    blk_start = (lax.broadcasted_iota(jnp.int32, (E, nbp), 1) * MOE_ROWS).astype(F32)
    owner = jnp.sum((blk_start >= pend).astype(jnp.int32), axis=0, keepdims=True)
    meta_ref[0:1, :] = jnp.minimum(owner, E - 1)
    meta_ref[1:2, :] = (owner < E).astype(jnp.int32)
    meta_ref[2:, :] = jnp.zeros((SUBLANES - 2, nbp), jnp.int32)


def _route(logits_t):
    E, N = logits_t.shape
    n_blocks = (N * TOP_K) // MOE_ROWS + N_EXPERTS
    nbp = -(-n_blocks // LANES) * LANES
    kn = jax.ShapeDtypeStruct((TOP_K, N), jnp.int32)
    top_i, gates, dest, meta = pl.pallas_call(
        functools.partial(_route_kernel, n_blocks),
        out_shape=[kn, jax.ShapeDtypeStruct((TOP_K, N), F32), kn,
                   jax.ShapeDtypeStruct((SUBLANES, nbp), jnp.int32)],
        scratch_shapes=[pltpu.VMEM((TOP_K, N), F32)],
        compiler_params=_params(),
    )(logits_t)
    del top_i
    return gates, dest, meta[0, :n_blocks], meta[1, :n_blocks], n_blocks


def _row_copy(src_ref, src_row, dst_ref, dst_row, sem):
    return pltpu.make_async_copy(src_ref.at[pl.ds(src_row, 1)], dst_ref.at[pl.ds(dst_row, 1)], sem)


def _dispatch_kernel(dest_ref, h_ref, buf_in_ref, buf_ref, sem):
    del buf_in_ref
    tm = h_ref.shape[0]
    n_tokens = dest_ref.shape[0] // TOP_K
    base = pl.program_id(0) * tm

    def issue(r, carry):
        for k in range(TOP_K):
            _row_copy(h_ref, r, buf_ref, dest_ref[k * n_tokens + base + r], sem).start()
        return carry

    lax.fori_loop(0, tm, issue, 0)

    def drain(r, carry):
        for k in range(TOP_K):
            _row_copy(h_ref, r, buf_ref, 0, sem).wait()
        return carry

    lax.fori_loop(0, tm, drain, 0)


def _dispatch(h, dest_flat, cap):
    N, D = h.shape
    tm = ROW_TILE
    zeros = jnp.zeros((cap, D), F32)
    grid_spec = pltpu.PrefetchScalarGridSpec(
        num_scalar_prefetch=1,
        grid=(N // tm,),
        in_specs=[pl.BlockSpec((tm, D), lambda i, dest: (i, 0)), pl.BlockSpec(memory_space=pl.ANY)],
        out_specs=pl.BlockSpec(memory_space=pl.ANY),
        scratch_shapes=[pltpu.SemaphoreType.DMA(())],
    )
    return pl.pallas_call(
        _dispatch_kernel,
        grid_spec=grid_spec,
        out_shape=jax.ShapeDtypeStruct((cap, D), F32),
        input_output_aliases={2: 0},
        compiler_params=_params(("arbitrary",)),
    )(dest_flat, h, zeros)


def _expert_kernel(blk_e_ref, blk_valid_ref, x_ref, wgu_ref, bgu_ref, wd_ref, bd_ref, y_ref, wgu_bf, wd_bf):
    i = pl.program_id(0)
    e = blk_e_ref[i]
    prev = blk_e_ref[jnp.maximum(i - 1, 0)]
    F = wd_ref.shape[1]
    strip = 256

    @pl.when(jnp.logical_or(i == 0, e != prev))
    def _():
        for j in range(wgu_bf.shape[1] // strip):
            wgu_bf[:, j * strip:(j + 1) * strip] = wgu_ref[0, :, j * strip:(j + 1) * strip].astype(BF16)
        for j in range(wd_bf.shape[1] // strip):
            wd_bf[:, j * strip:(j + 1) * strip] = wd_ref[0, :, j * strip:(j + 1) * strip].astype(BF16)

    @pl.when(blk_valid_ref[i] == 1)
    def _():
        x = x_ref[...].astype(BF16)
        hgu = jnp.dot(x, wgu_bf[...], preferred_element_type=F32) + bgu_ref[0]
        gate = jnp.minimum(hgu[:, :F], SWIGLU_LIMIT)
        up = jnp.clip(hgu[:, F:], -SWIGLU_LIMIT, SWIGLU_LIMIT)
        act = (up + 1.0) * gate * _sigmoid(SWIGLU_ALPHA * gate)
        y_ref[...] = jnp.dot(act.astype(BF16), wd_bf[...], preferred_element_type=F32) + bd_ref[0]

    @pl.when(blk_valid_ref[i] == 0)
    def _():
        y_ref[...] = jnp.zeros_like(y_ref)


def _experts(buf, blk_e, blk_valid, w_gu, b_gu, w_down, b_down):
    cap, D = buf.shape
    E, _, F2 = w_gu.shape
    F = F2 // 2
    rb = MOE_ROWS
    grid_spec = pltpu.PrefetchScalarGridSpec(
        num_scalar_prefetch=2,
        grid=(cap // rb,),
        in_specs=[
            pl.BlockSpec((rb, D), lambda i, be, bv: (i, 0)),
            pl.BlockSpec((1, D, F2), lambda i, be, bv: (be[i], 0, 0)),
            pl.BlockSpec((1, 1, F2), lambda i, be, bv: (be[i], 0, 0)),
            pl.BlockSpec((1, F, D), lambda i, be, bv: (be[i], 0, 0)),
            pl.BlockSpec((1, 1, D), lambda i, be, bv: (be[i], 0, 0)),
        ],
        out_specs=pl.BlockSpec((rb, D), lambda i, be, bv: (i, 0)),
        scratch_shapes=[pltpu.VMEM((D, F2), BF16), pltpu.VMEM((F, D), BF16)],
    )
    return pl.pallas_call(
        _expert_kernel,
        grid_spec=grid_spec,
        out_shape=jax.ShapeDtypeStruct((cap, D), F32),
        compiler_params=_params(("arbitrary",)),
    )(blk_e, blk_valid, buf, w_gu, b_gu.reshape(E, 1, F2), w_down, b_down.reshape(E, 1, D))


def _combine_kernel(has_next, dest_ref, ybuf_ref, gates_ref, *refs):
    tail_refs = refs[:6]
    if has_next:
        xo_ref, ho_ref, rows_ref, sem = refs[6:]
    else:
        xo_ref, rows_ref, sem = refs[6:]
    tm = xo_ref.shape[0]
    n_tokens = dest_ref.shape[0] // TOP_K
    base = pl.program_id(0) * tm

    def issue(r, carry):
        for k in range(TOP_K):
            _row_copy(ybuf_ref, dest_ref[k * n_tokens + base + r], rows_ref.at[k], r, sem).start()
        return carry

    lax.fori_loop(0, tm, issue, 0)

    def drain(r, carry):
        for k in range(TOP_K):
            _row_copy(ybuf_ref, 0, rows_ref.at[k], r, sem).wait()
        return carry

    lax.fori_loop(0, tm, drain, 0)
    gates = gates_ref[...]
    y = gates[:, 0:1] * rows_ref[0]
    for k in range(1, TOP_K):
        y = y + gates[:, k:k + 1] * rows_ref[k]
    x_ref, gate_ref, lng_ref, lnb_ref, sc_ref, sh_ref = tail_refs
    x_new = _layernorm(DEEPNORM_ALPHA * x_ref[...] + gate_ref[0] * y) * lng_ref[...] + lnb_ref[...]
    xo_ref[...] = x_new
    if has_next:
        ho_ref[...] = (_layernorm(x_new) * (1.0 + sc_ref[0]) + sh_ref[0]).astype(BF16)


def _combine(ybuf, dest_flat, gates_nk, tail, L, has_next):
    N = gates_nk.shape[0]
    D = ybuf.shape[1]
    tm = min(ROW_TILE, L)
    tpb = L // tm
    row = pl.BlockSpec((tm, D), lambda i, dest: (i, 0))
    per_batch = pl.BlockSpec((1, 1, D), lambda i, dest: (i // tpb, 0, 0))
    vec = pl.BlockSpec((1, D), lambda i, dest: (0, 0))
    out_specs = [row, row] if has_next else [row]
    out_shape = [jax.ShapeDtypeStruct((N, D), F32)] + ([jax.ShapeDtypeStruct((N, D), BF16)] if has_next else [])
    grid_spec = pltpu.PrefetchScalarGridSpec(
        num_scalar_prefetch=1,
        grid=(N // tm,),
        in_specs=[pl.BlockSpec(memory_space=pl.ANY), pl.BlockSpec((tm, TOP_K), lambda i, dest: (i, 0)),
                  row, per_batch, vec, vec, per_batch, per_batch],
        out_specs=out_specs,
        scratch_shapes=[pltpu.VMEM((TOP_K, tm, D), F32), pltpu.SemaphoreType.DMA(())],
    )
    return pl.pallas_call(
        functools.partial(_combine_kernel, has_next),
        grid_spec=grid_spec,
        out_shape=out_shape,
        compiler_params=_params(("arbitrary",)),
    )(dest_flat, ybuf, gates_nk, *tail.args)


def _moe(h, logits_t, w_gu, b_gu, w_down, b_down, tail, L, has_next):
    N, D = h.shape
    gates, dest, blk_e, blk_valid, n_blocks = _route(logits_t)
    dest_flat = dest.reshape(-1)
    buf = _dispatch(h, dest_flat, n_blocks * MOE_ROWS)
    ybuf = _experts(buf, blk_e, blk_valid, w_gu, b_gu, w_down, b_down)
    return _combine(ybuf, dest_flat, gates.T, tail, L, has_next)


def kernel(x, c, ada_w, ada_b, post_ln_g, post_ln_b, fnet_w_out, fnet_b_out, attn_w_qkv, attn_q_norm, attn_k_norm, attn_w_out, s5_a_re, s5_a_im, s5_log_dt, s5_b_re, s5_b_im, s5_c_re, s5_c_im, s5_d, s5_w_glu, s5_w_out, hg_w_in, hg_lb, hg_norm, hg_w_out, moe_w_router, moe_b_router, moe_w_gu, moe_b_gu, moe_w_down, moe_b_down):
    B, L, D = x.shape
    N = B * L
    mod = _modulation(c, ada_w, ada_b)
    lb_soft = jax.nn.softmax(hg_lb.astype(F32), axis=0)
    lb_all = jnp.cumsum(lb_soft, axis=0) - lb_soft[0]
    x2 = x.reshape(N, D)
    h = _ln_mod(x2, mod[0, :, 1], mod[0, :, 0], L)
    for i in range(DEPTH):
        sh1, sc1, g1, sh2, sc2, g2 = (mod[i, :, j] for j in range(6))
        del sh1, sc1
        m, j = i % N_MIXERS, i // N_MIXERS
        tail = _Tail(x2, g1, post_ln_g[i, 0], post_ln_b[i, 0], sc2, sh2)
        wrt = moe_w_router[i].T
        br = moe_b_router[i].reshape(N_EXPERTS, 1)
        if m == 0:
            x2, hm, lg = _fnet_mixer(h, B, L, fnet_w_out[j], fnet_b_out[j], tail, wrt, br)
        elif m == 1:
            x2, hm, lg = _attention_mixer(h, B, L, attn_w_qkv[j], attn_q_norm[j], attn_k_norm[j], attn_w_out[j],
                                          tail, wrt, br)
        elif m == 2:
            x2, hm, lg = _s5_mixer(h, B, L, s5_a_re[j], s5_a_im[j], s5_log_dt[j], s5_b_re[j], s5_b_im[j],
                                   s5_c_re[j], s5_c_im[j], s5_d[j], s5_w_glu[j], s5_w_out[j], tail, wrt, br)
        else:
            x2, hm, lg = _hgrn2_mixer(h, B, L, hg_w_in[j], lb_all[i], hg_norm[j], hg_w_out[j], tail, wrt, br)
        has_next = i + 1 < DEPTH
        nxt = i + 1 if has_next else i
        tail = _Tail(x2, g2, post_ln_g[i, 1], post_ln_b[i, 1], mod[nxt, :, 1], mod[nxt, :, 0])
        outs = _moe(hm, lg, moe_w_gu[i], moe_b_gu[i], moe_w_down[i], moe_b_down[i], tail, L, has_next)
        if has_next:
            x2, h = outs
        else:
            (x2,) = outs
    return x2.reshape(B, L, D)
```
